```python
import math
import jax, jax.numpy as jnp
from jax import lax
import numpy as np

D_MODEL = 1024
BATCH = 4
SEQ = 8192
DEPTH = 1

HEAD_DIM = 64
ATTN_HEADS = 8
KV_HEADS = 2
Q_PER_KV = ATTN_HEADS // KV_HEADS
WINDOW = 128
BLOCK = 128
ATTN_WIDTH = ATTN_HEADS * HEAD_DIM
KV_WIDTH = KV_HEADS * HEAD_DIM
POOL_WINDOWS = (2, 4, 8, 16)
POOL_GROUPS = len(POOL_WINDOWS)
POOL_WIDTH = D_MODEL - ATTN_WIDTH
POOL_GROUP_DIM = POOL_WIDTH // POOL_GROUPS
MIX_WIDTH = ATTN_WIDTH + POOL_WIDTH
IN_WIDTH = ATTN_WIDTH + 2 * KV_WIDTH + POOL_WIDTH
D_FF = ((8 * D_MODEL // 3 + 127) // 128) * 128
N_MOD = 9
ALPHA = (2.0 * DEPTH) ** 0.25
BETA = (8.0 * DEPTH) ** -0.25
LN_EPS = 1e-5
NEG_INF = -1e30

kernel_name = "hybrid_swa_sink_pool_macaron_deepnorm_adaln"


def _ln_stats(x):
    xf = x.astype(jnp.float32)
    mu = jnp.mean(xf, axis=-1, keepdims=True)
    var = jnp.mean(jnp.square(xf - mu), axis=-1, keepdims=True)
    return (xf - mu) * lax.rsqrt(var + LN_EPS)


def layer_norm(x, g, b):
    y = _ln_stats(x) * g.astype(jnp.float32) + b.astype(jnp.float32)
    return y.astype(x.dtype)


def modulate(x, shift, scale):
    y = _ln_stats(x) * (1.0 + scale[:, None, :].astype(jnp.float32)) + shift[:, None, :].astype(jnp.float32)
    return y.astype(x.dtype)


def swiglu(h, w_gate, w_up, w_down):
    return (jax.nn.silu(h @ w_gate) * (h @ w_up)) @ w_down


def sliding_window_sink_attention(q, k, v, sinks):
    b, s = q.shape[0], q.shape[1]
    nb = s // BLOCK
    qb = q.reshape(b, nb, BLOCK, KV_HEADS, Q_PER_KV, HEAD_DIM)
    kb = k.reshape(b, nb, BLOCK, KV_HEADS, HEAD_DIM)
    vb = v.reshape(b, nb, BLOCK, KV_HEADS, HEAD_DIM)
    pad = ((0, 0), (1, 0), (0, 0), (0, 0), (0, 0))
    kcat = jnp.concatenate([jnp.pad(kb, pad)[:, :-1], kb], axis=2)
    vcat = jnp.concatenate([jnp.pad(vb, pad)[:, :-1], vb], axis=2)
    scores = jnp.einsum('bnqkgd,bnjkd->bnkgqj', qb, kcat).astype(jnp.float32) * (HEAD_DIM ** -0.5)
    qpos = jnp.arange(BLOCK)[:, None] + BLOCK
    kpos = jnp.arange(2 * BLOCK)[None, :]
    rel = qpos - kpos
    band = (rel >= 0) & (rel < WINDOW)
    blk = jnp.arange(nb)[:, None, None]
    valid = band[None] & ((blk > 0) | (kpos >= BLOCK)[None])
    scores = jnp.where(valid[None, :, None, None], scores, NEG_INF)
    sink = sinks.astype(jnp.float32).reshape(1, 1, KV_HEADS, Q_PER_KV, 1, 1)
    m = jnp.maximum(jnp.max(scores, axis=-1, keepdims=True), sink)
    p = jnp.exp(scores - m)
    probs = p / (jnp.sum(p, axis=-1, keepdims=True) + jnp.exp(sink - m))
    out = jnp.einsum('bnkgqj,bnjkd->bnqkgd', probs.astype(v.dtype), vcat)
    return out.reshape(b, s, ATTN_WIDTH)


def multiscale_pool(p, w_mix, ch_scale):
    b, s, _ = p.shape
    pf = p.astype(jnp.float32)
    cs = jnp.pad(jnp.cumsum(pf, axis=1), ((0, 0), (1, 0), (0, 0)))
    t = jnp.arange(s)
    pooled = []
    for g, w in enumerate(POOL_WINDOWS):
        sl = slice(g * POOL_GROUP_DIM, (g + 1) * POOL_GROUP_DIM)
        csg = cs[:, :, sl]
        lo = jnp.take(csg, jnp.maximum(t + 1 - w, 0), axis=1)
        cnt = jnp.minimum(t + 1, w).astype(jnp.float32)[None, :, None]
        pooled.append((csg[:, 1:] - lo) / cnt - pf[:, :, sl])
    pooled = jnp.stack(pooled, axis=2).astype(p.dtype)
    mixed = jnp.einsum('bsgc,gcd->bsgd', pooled, w_mix)
    return mixed.reshape(b, s, POOL_WIDTH) * ch_scale


def token_mix(h, w_in, sinks, pool_w_mix, pool_scale, w_out):
    b, s, _ = h.shape
    proj = h @ w_in
    q = proj[..., :ATTN_WIDTH].reshape(b, s, ATTN_HEADS, HEAD_DIM)
    k = proj[..., ATTN_WIDTH:ATTN_WIDTH + KV_WIDTH].reshape(b, s, KV_HEADS, HEAD_DIM)
    v = proj[..., ATTN_WIDTH + KV_WIDTH:ATTN_WIDTH + 2 * KV_WIDTH].reshape(b, s, KV_HEADS, HEAD_DIM)
    pin = proj[..., ATTN_WIDTH + 2 * KV_WIDTH:]
    a = sliding_window_sink_attention(q, k, v, sinks)
    m = multiscale_pool(pin, pool_w_mix, pool_scale)
    return jnp.concatenate([a, m], axis=-1) @ w_out


def setup_inputs(seed: int = 0) -> dict:
    key = jax.random.key(seed)
    ks = jax.random.split(key, 24)
    L, D, F = DEPTH, D_MODEL, D_FF
    nrm = lambda k, shape, s: jax.random.normal(k, shape, jnp.float32) * s
    col_scale = jnp.concatenate([
        jnp.ones((ATTN_WIDTH + KV_WIDTH,), jnp.float32),
        jnp.full((KV_WIDTH + POOL_WIDTH,), BETA, jnp.float32)])
    return {
        "x": nrm(ks[0], (BATCH, SEQ, D), 1.0),
        "c": nrm(ks[1], (BATCH, D), 1.0),
        "w_ada": nrm(ks[2], (L, D, N_MOD * D), 0.5 * D ** -0.5),
        "b_ada": nrm(ks[3], (L, N_MOD * D), 0.02),
        "ffn1_w_gate": nrm(ks[4], (L, D, F), D ** -0.5),
        "ffn1_w_up": nrm(ks[5], (L, D, F), BETA * D ** -0.5),
        "ffn1_w_down": nrm(ks[6], (L, F, D), BETA * F ** -0.5),
        "ln1_g": 1.0 + nrm(ks[7], (L, D), 0.02),
        "ln1_b": nrm(ks[8], (L, D), 0.02),
        "w_in": nrm(ks[9], (L, D, IN_WIDTH), D ** -0.5) * col_scale,
        "attn_sinks": nrm(ks[10], (L, ATTN_HEADS), 0.5),
        "pool_w_mix": nrm(ks[11], (L, POOL_GROUPS, POOL_GROUP_DIM, POOL_GROUP_DIM), POOL_GROUP_DIM ** -0.5),
        "pool_scale": 1.0 + nrm(ks[12], (L, POOL_WIDTH), 0.1),
        "w_out": nrm(ks[13], (L, MIX_WIDTH, D), BETA * MIX_WIDTH ** -0.5),
        "ln2_g": 1.0 + nrm(ks[14], (L, D), 0.02),
        "ln2_b": nrm(ks[15], (L, D), 0.02),
        "ffn2_w_gate": nrm(ks[16], (L, D, F), D ** -0.5),
        "ffn2_w_up": nrm(ks[17], (L, D, F), BETA * D ** -0.5),
        "ffn2_w_down": nrm(ks[18], (L, F, D), BETA * F ** -0.5),
        "ln3_g": 1.0 + nrm(ks[19], (L, D), 0.02),
        "ln3_b": nrm(ks[20], (L, D), 0.02),
    }


def reference(x, c, w_ada, b_ada, ffn1_w_gate, ffn1_w_up, ffn1_w_down, ln1_g, ln1_b,
              w_in, attn_sinks, pool_w_mix, pool_scale, w_out, ln2_g, ln2_b,
              ffn2_w_gate, ffn2_w_up, ffn2_w_down, ln3_g, ln3_b):
    b = x.shape[0]
    for l in range(DEPTH):
        mod = (jax.nn.silu(c) @ w_ada[l] + b_ada[l]).reshape(b, N_MOD, D_MODEL)
        sh1, sc1, g1 = mod[:, 0], mod[:, 1], mod[:, 2]
        sh2, sc2, g2 = mod[:, 3], mod[:, 4], mod[:, 5]
        sh3, sc3, g3 = mod[:, 6], mod[:, 7], mod[:, 8]
        h = modulate(x, sh1, sc1)
        y = swiglu(h, ffn1_w_gate[l], ffn1_w_up[l], ffn1_w_down[l])
        x = layer_norm(ALPHA * x + 0.5 * g1[:, None, :] * y, ln1_g[l], ln1_b[l])
        h = modulate(x, sh2, sc2)
        y = token_mix(h, w_in[l], attn_sinks[l], pool_w_mix[l], pool_scale[l], w_out[l])
        x = layer_norm(ALPHA * x + g2[:, None, :] * y, ln2_g[l], ln2_b[l])
        h = modulate(x, sh3, sc3)
        y = swiglu(h, ffn2_w_gate[l], ffn2_w_up[l], ffn2_w_down[l])
        x = layer_norm(ALPHA * x + 0.5 * g3[:, None, :] * y, ln3_g[l], ln3_b[l])
    return x
```

```python
import functools

import jax
import jax.numpy as jnp
from jax import lax
from jax.experimental import pallas as pl
from jax.experimental.pallas import tpu as pltpu

D_MODEL = 1024
HEAD_DIM = 64
ATTN_HEADS = 8
KV_HEADS = 2
Q_PER_KV = ATTN_HEADS // KV_HEADS
WINDOW = 128
BLOCK = 128
ATTN_WIDTH = ATTN_HEADS * HEAD_DIM
KV_WIDTH = KV_HEADS * HEAD_DIM
POOL_WINDOWS = (2, 4, 8, 16)
POOL_GROUPS = len(POOL_WINDOWS)
POOL_WIDTH = D_MODEL - ATTN_WIDTH
POOL_GROUP_DIM = POOL_WIDTH // POOL_GROUPS
IN_WIDTH = ATTN_WIDTH + 2 * KV_WIDTH + POOL_WIDTH
N_MOD = 9
DEPTH = 1
ALPHA = (2.0 * DEPTH) ** 0.25
LN_EPS = 1e-5
NEG_INF = -1e30

LANES = 128
V7X_VMEM_BYTES = 64 * 1024 * 1024
POOL_HALO = 16

ROW_TILE = 512
ADA_COL_TILE = 1152

F32 = jnp.float32
BF16 = jnp.bfloat16


def _normalize(x):
    mu = jnp.mean(x, axis=-1, keepdims=True)
    xc = x - mu
    var = jnp.mean(xc * xc, axis=-1, keepdims=True)
    return xc * lax.rsqrt(var + LN_EPS)


def _silu(x):
    return x * jax.nn.sigmoid(x)


def _resident(shape):
    zeros = (0,) * len(shape)
    return pl.BlockSpec(shape, lambda i: zeros, pipeline_mode=pl.Buffered(1))


def _per_batch(tiles_per_batch):
    return pl.BlockSpec((None, 1, D_MODEL), lambda i: (i // tiles_per_batch, 0, 0))


def _ada_kernel(c_ref, w_ref, b_ref, o_ref):
    s = _silu(c_ref[...]).astype(BF16)
    o_ref[...] = jnp.dot(s, w_ref[...].astype(BF16), preferred_element_type=F32) + b_ref[...]


def _ada_modulation(c, w_ada, b_ada):
    b, d = c.shape
    n = w_ada.shape[1]
    assert n % ADA_COL_TILE == 0 and ADA_COL_TILE % LANES == 0
    return pl.pallas_call(
        _ada_kernel,
        grid=(n // ADA_COL_TILE,),
        in_specs=[
            pl.BlockSpec((b, d), lambda j: (0, 0)),
            pl.BlockSpec((d, ADA_COL_TILE), lambda j: (0, j)),
            pl.BlockSpec((1, ADA_COL_TILE), lambda j: (0, j)),
        ],
        out_specs=pl.BlockSpec((b, ADA_COL_TILE), lambda j: (0, j)),
        out_shape=jax.ShapeDtypeStruct((b, n), F32),
        compiler_params=pltpu.CompilerParams(dimension_semantics=("arbitrary",)),
        name="ada_modulation",
    )(c, w_ada, b_ada.reshape(1, n))


def _ffn_kernel(x_ref, sh_ref, sc_ref, g_ref, wg_ref, wu_ref, wd_ref, lng_ref, lnb_ref, o_ref):
    x = x_ref[...]
    h = (_normalize(x) * (1.0 + sc_ref[...]) + sh_ref[...]).astype(BF16)
    gate = jnp.dot(h, wg_ref[...], preferred_element_type=F32)
    up = jnp.dot(h, wu_ref[...], preferred_element_type=F32)
    act = (_silu(gate) * up).astype(BF16)
    y = jnp.dot(act, wd_ref[...], preferred_element_type=F32)
    z = ALPHA * x + (0.5 * g_ref[...]) * y
    o_ref[...] = _normalize(z) * lng_ref[...] + lnb_ref[...]


def _ffn_vmem_bytes(tm, d, f):
    io = 2 * 2 * tm * d * 4
    weights = 3 * d * f * 2
    temps = tm * d * (2 + 4 + 4) + tm * f * (4 + 4 + 2)
    return io + weights + temps


def _ffn_half_step(x, shift, scale, gate, w_gate, w_up, w_down, ln_g, ln_b, seq):
    n, d = x.shape
    f = w_gate.shape[1]
    tm = ROW_TILE
    assert n % tm == 0 and seq % tm == 0
    tiles_per_batch = seq // tm
    vmem = min(_ffn_vmem_bytes(tm, d, f) * 5 // 4, V7X_VMEM_BYTES * 7 // 8)
    row = pl.BlockSpec((tm, d), lambda i: (i, 0))
    vec = _per_batch(tiles_per_batch)
    return pl.pallas_call(
        _ffn_kernel,
        grid=(n // tm,),
        in_specs=[row, vec, vec, vec, _resident((d, f)), _resident((d, f)), _resident((f, d)),
                  _resident((1, d)), _resident((1, d))],
        out_specs=row,
        out_shape=jax.ShapeDtypeStruct((n, d), F32),
        compiler_params=pltpu.CompilerParams(dimension_semantics=("arbitrary",), vmem_limit_bytes=vmem),
        name="ffn_half_step",
    )(x, shift, scale, gate, w_gate, w_up, w_down, ln_g, ln_b)


def _mix_kernel(tiles_per_batch, x_ref, sh_ref, sc_ref, g_ref, win_ref, sinks_ref, wmix_ref, pscale_ref,
                wout_ref, lng_ref, lnb_ref, o_ref, q_ref, k_ref, v_ref, p_ref, cat_ref):
    tm = x_ref.shape[0]
    tile_in_seq = pl.program_id(0) % tiles_per_batch
    first = tile_in_seq == 0

    @pl.when(first)
    def _():
        k_ref[:, 0:BLOCK, :] = jnp.zeros((4, BLOCK, LANES), BF16)
        v_ref[:, 0:BLOCK, :] = jnp.zeros((4, BLOCK, LANES), BF16)
        p_ref[0:POOL_HALO, :] = jnp.zeros((POOL_HALO, POOL_WIDTH), F32)

    x = x_ref[...]
    h = (_normalize(x) * (1.0 + sc_ref[...]) + sh_ref[...]).astype(BF16)
    proj = jnp.dot(h, win_ref[...], preferred_element_type=F32)

    q_ref[...] = (proj[:, :ATTN_WIDTH] * (HEAD_DIM ** -0.5)).astype(BF16)
    k = proj[:, ATTN_WIDTH:ATTN_WIDTH + KV_WIDTH]
    v = proj[:, ATTN_WIDTH + KV_WIDTH:ATTN_WIDTH + 2 * KV_WIDTH]
    low_half = lax.broadcasted_iota(jnp.int32, (tm, LANES), 1) < HEAD_DIM
    for ref, val in ((k_ref, k), (v_ref, v)):
        swapped = pltpu.roll(val, HEAD_DIM, 1)
        ref[0, BLOCK:, :] = jnp.where(low_half, val, 0.0).astype(BF16)
        ref[1, BLOCK:, :] = jnp.where(low_half, 0.0, swapped).astype(BF16)
        ref[2, BLOCK:, :] = jnp.where(low_half, swapped, 0.0).astype(BF16)
        ref[3, BLOCK:, :] = jnp.where(low_half, 0.0, val).astype(BF16)
    p_ref[POOL_HALO:, :] = proj[:, ATTN_WIDTH + 2 * KV_WIDTH:]

    qi = lax.broadcasted_iota(jnp.int32, (BLOCK, 2 * BLOCK), 0)
    kp = lax.broadcasted_iota(jnp.int32, (BLOCK, 2 * BLOCK), 1)
    rel = qi + BLOCK - kp
    band = (rel >= 0) & (rel < WINDOW)
    first_key = jnp.where(first, BLOCK, 0)
    band_first = band & (kp >= first_key)
    contract_last = (((1,), (1,)), ((), ()))
    for j in range(tm // BLOCK):
        valid = band_first if j == 0 else band
        rows = slice(j * BLOCK, (j + 1) * BLOCK)
        keys = slice(j * BLOCK, (j + 2) * BLOCK)
        for pair in range(ATTN_HEADS // 2):
            kv = pair // (Q_PER_KV // 2)
            cols = slice(pair * LANES, (pair + 1) * LANES)
            q_pair = q_ref[rows, cols]
            out = jnp.zeros((BLOCK, LANES), F32)
            for parity in range(2):
                sink = sinks_ref[2 * pair + parity]
                s = lax.dot_general(q_pair, k_ref[2 * kv + parity, keys, :], contract_last,
                                    preferred_element_type=F32)
                s = jnp.where(valid, s, NEG_INF)
                m = jnp.maximum(jnp.max(s, axis=-1, keepdims=True), sink)
                p = jnp.exp(s - m)
                denom = jnp.sum(p, axis=-1, keepdims=True) + jnp.exp(sink - m)
                probs = (p / denom).astype(BF16)
                out = out + jnp.dot(probs, v_ref[2 * kv + parity, keys, :], preferred_element_type=F32)
            cat_ref[rows, cols] = out.astype(BF16)

    pos = tile_in_seq * tm + lax.broadcasted_iota(jnp.int32, (tm, 1), 0)
    for g, w in enumerate(POOL_WINDOWS):
        cols = slice(g * POOL_GROUP_DIM, (g + 1) * POOL_GROUP_DIM)
        own = p_ref[POOL_HALO:, cols]
        total = own
        for back in range(1, w):
            total = total + p_ref[POOL_HALO - back:POOL_HALO - back + tm, cols]
        count = jnp.minimum(pos + 1, w).astype(F32)
        pooled = (total / count - own).astype(BF16)
        mixed = jnp.dot(pooled, wmix_ref[g], preferred_element_type=F32) * pscale_ref[:, cols]
        cat_ref[:, ATTN_WIDTH + g * POOL_GROUP_DIM:ATTN_WIDTH + (g + 1) * POOL_GROUP_DIM] = mixed.astype(BF16)

    k_ref[:, 0:BLOCK, :] = k_ref[:, tm:tm + BLOCK, :]
    v_ref[:, 0:BLOCK, :] = v_ref[:, tm:tm + BLOCK, :]
    p_ref[0:POOL_HALO, :] = p_ref[tm:tm + POOL_HALO, :]

    y = jnp.dot(cat_ref[...], wout_ref[...], preferred_element_type=F32)
    z = ALPHA * x + g_ref[...] * y
    o_ref[...] = _normalize(z) * lng_ref[...] + lnb_ref[...]


def _mix_vmem_bytes(tm, d):
    io = 2 * 2 * tm * d * 4
    weights = (d * IN_WIDTH + d * d + POOL_GROUPS * POOL_GROUP_DIM * POOL_GROUP_DIM) * 2
    scratch = tm * ATTN_WIDTH * 2 + 2 * 4 * (BLOCK + tm) * LANES * 2 + (POOL_HALO + tm) * POOL_WIDTH * 4 + tm * d * 2
    temps = tm * d * (2 + 4 + 4) + tm * IN_WIDTH * 4 + 4 * tm * LANES * 4
    return io + weights + scratch + temps


def _token_mix(x, shift, scale, gate, w_in, sinks, w_mix, pool_scale, w_out, ln_g, ln_b, seq):
    n, d = x.shape
    tm = ROW_TILE
    assert n % tm == 0 and seq % tm == 0 and tm % BLOCK == 0
    tiles_per_batch = seq // tm
    vmem = min(_mix_vmem_bytes(tm, d) * 3 // 2, V7X_VMEM_BYTES * 7 // 8)
    row = pl.BlockSpec((tm, d), lambda i: (i, 0))
    vec = _per_batch(tiles_per_batch)
    return pl.pallas_call(
        functools.partial(_mix_kernel, tiles_per_batch),
        grid=(n // tm,),
        in_specs=[row, vec, vec, vec, _resident((d, IN_WIDTH)),
                  pl.BlockSpec(memory_space=pltpu.SMEM),
                  _resident((POOL_GROUPS, POOL_GROUP_DIM, POOL_GROUP_DIM)), _resident((1, POOL_WIDTH)),
                  _resident((d, d)), _resident((1, d)), _resident((1, d))],
        out_specs=row,
        out_shape=jax.ShapeDtypeStruct((n, d), F32),
        scratch_shapes=[
            pltpu.VMEM((tm, ATTN_WIDTH), BF16),
            pltpu.VMEM((4, BLOCK + tm, LANES), BF16),
            pltpu.VMEM((4, BLOCK + tm, LANES), BF16),
            pltpu.VMEM((POOL_HALO + tm, POOL_WIDTH), F32),
            pltpu.VMEM((tm, d), BF16),
        ],
        compiler_params=pltpu.CompilerParams(dimension_semantics=("arbitrary",), vmem_limit_bytes=vmem),
        name="token_mix",
    )(x, shift, scale, gate, w_in, sinks, w_mix, pool_scale, w_out, ln_g, ln_b)


def kernel(x, c, w_ada, b_ada, ffn1_w_gate, ffn1_w_up, ffn1_w_down, ln1_g, ln1_b, w_in, attn_sinks,
           pool_w_mix, pool_scale, w_out, ln2_g, ln2_b, ffn2_w_gate, ffn2_w_up, ffn2_w_down, ln3_g, ln3_b):
    b, s, d = x.shape
    x = x.reshape(b * s, d)
    for l in range(DEPTH):
        mod = _ada_modulation(c, w_ada[l], b_ada[l]).reshape(b, N_MOD, 1, d)
        sh1, sc1, g1, sh2, sc2, g2, sh3, sc3, g3 = (mod[:, i] for i in range(N_MOD))
        row = lambda a: a[l].reshape(1, -1)
        x = _ffn_half_step(x, sh1, sc1, g1, ffn1_w_gate[l].astype(BF16), ffn1_w_up[l].astype(BF16),
                           ffn1_w_down[l].astype(BF16), row(ln1_g), row(ln1_b), s)
        x = _token_mix(x, sh2, sc2, g2, w_in[l].astype(BF16), attn_sinks[l], pool_w_mix[l].astype(BF16),
                       row(pool_scale), w_out[l].astype(BF16), row(ln2_g), row(ln2_b), s)
        x = _ffn_half_step(x, sh3, sc3, g3, ffn2_w_gate[l].astype(BF16), ffn2_w_up[l].astype(BF16),
                           ffn2_w_down[l].astype(BF16), row(ln3_g), row(ln3_b), s)
    return x.reshape(b, s, d)
```

```python
import functools

import jax
import jax.numpy as jnp
from jax import lax
from jax.experimental import pallas as pl
from jax.experimental.pallas import tpu as pltpu

D_MODEL = 1024
HEAD_DIM = 64
ATTN_HEADS = 8
KV_HEADS = 2
Q_PER_KV = ATTN_HEADS // KV_HEADS
WINDOW = 128
BLOCK = 128
ATTN_WIDTH = ATTN_HEADS * HEAD_DIM
KV_WIDTH = KV_HEADS * HEAD_DIM
POOL_WINDOWS = (2, 4, 8, 16)
POOL_GROUPS = len(POOL_WINDOWS)
POOL_WIDTH = D_MODEL - ATTN_WIDTH
POOL_GROUP_DIM = POOL_WIDTH // POOL_GROUPS
IN_WIDTH = ATTN_WIDTH + 2 * KV_WIDTH + POOL_WIDTH
N_MOD = 9
DEPTH = 1
ALPHA = (2.0 * DEPTH) ** 0.25
LN_EPS = 1e-5
NEG_INF = -1e30

LANES = 128
SUBLANES = 8
V7X_VMEM_BYTES = 64 * 1024 * 1024
POOL_HALO = SUBLANES * (max(POOL_WINDOWS).bit_length() - 1)

ROW_TILE = 512
SOFTMAX_ROWS = 32
ADA_COL_TILE = 1152

F32 = jnp.float32
BF16 = jnp.bfloat16


def _normalize(x):
    mu = jnp.mean(x, axis=-1, keepdims=True)
    xc = x - mu
    var = jnp.mean(xc * xc, axis=-1, keepdims=True)
    return xc * lax.rsqrt(var + LN_EPS)


def _silu(x):
    return x * jax.nn.sigmoid(x)


def _resident(shape):
    zeros = (0,) * len(shape)
    return pl.BlockSpec(shape, lambda i: zeros, pipeline_mode=pl.Buffered(1))


def _per_batch(tiles_per_batch):
    return pl.BlockSpec((None, 1, D_MODEL), lambda i: (i // tiles_per_batch, 0, 0))


def _ada_kernel(c_ref, w_ref, b_ref, o_ref):
    s = _silu(c_ref[...]).astype(BF16)
    o_ref[...] = jnp.dot(s, w_ref[...].astype(BF16), preferred_element_type=F32) + b_ref[...]


def _ada_modulation(c, w_ada, b_ada):
    b, d = c.shape
    n = w_ada.shape[1]
    assert n % ADA_COL_TILE == 0 and ADA_COL_TILE % LANES == 0
    return pl.pallas_call(
        _ada_kernel,
        grid=(n // ADA_COL_TILE,),
        in_specs=[
            pl.BlockSpec((b, d), lambda j: (0, 0)),
            pl.BlockSpec((d, ADA_COL_TILE), lambda j: (0, j)),
            pl.BlockSpec((1, ADA_COL_TILE), lambda j: (0, j)),
        ],
        out_specs=pl.BlockSpec((b, ADA_COL_TILE), lambda j: (0, j)),
        out_shape=jax.ShapeDtypeStruct((b, n), F32),
        compiler_params=pltpu.CompilerParams(dimension_semantics=("arbitrary",)),
        name="ada_modulation",
    )(c, w_ada, b_ada.reshape(1, n))


def _ffn_kernel(x_ref, sh_ref, sc_ref, g_ref, wg_ref, wu_ref, wd_ref, lng_ref, lnb_ref, o_ref):
    x = x_ref[...]
    h = (_normalize(x) * (1.0 + sc_ref[...]) + sh_ref[...]).astype(BF16)
    gate = jnp.dot(h, wg_ref[...], preferred_element_type=F32)
    up = jnp.dot(h, wu_ref[...], preferred_element_type=F32)
    act = (_silu(gate) * up).astype(BF16)
    y = jnp.dot(act, wd_ref[...], preferred_element_type=F32)
    z = ALPHA * x + (0.5 * g_ref[...]) * y
    o_ref[...] = _normalize(z) * lng_ref[...] + lnb_ref[...]


def _ffn_vmem_bytes(tm, d, f):
    io = 2 * 2 * tm * d * 4
    weights = 3 * d * f * 2
    temps = tm * d * (2 + 4 + 4) + tm * f * (4 + 4 + 2)
    return io + weights + temps


def _ffn_half_step(x, shift, scale, gate, w_gate, w_up, w_down, ln_g, ln_b, seq):
    n, d = x.shape
    f = w_gate.shape[1]
    tm = ROW_TILE
    assert n % tm == 0 and seq % tm == 0
    tiles_per_batch = seq // tm
    vmem = min(_ffn_vmem_bytes(tm, d, f) * 5 // 4, V7X_VMEM_BYTES * 7 // 8)
    row = pl.BlockSpec((tm, d), lambda i: (i, 0))
    vec = _per_batch(tiles_per_batch)
    return pl.pallas_call(
        _ffn_kernel,
        grid=(n // tm,),
        in_specs=[row, vec, vec, vec, _resident((d, f)), _resident((d, f)), _resident((f, d)),
                  _resident((1, d)), _resident((1, d))],
        out_specs=row,
        out_shape=jax.ShapeDtypeStruct((n, d), F32),
        compiler_params=pltpu.CompilerParams(dimension_semantics=("arbitrary",), vmem_limit_bytes=vmem),
        name="ffn_half_step",
    )(x, shift, scale, gate, w_gate, w_up, w_down, ln_g, ln_b)


def _mix_kernel(tiles_per_batch, x_ref, sh_ref, sc_ref, g_ref, win_ref, sinks_ref, wmix_ref, pscale_ref,
                wout_ref, lng_ref, lnb_ref, o_ref, q_ref, k_ref, vt_ref, s_ref, pt_ref, band_ref, p_ref, lvl_ref,
                cat_ref):
    tm = x_ref.shape[0]
    tile_in_seq = pl.program_id(0) % tiles_per_batch
    first = tile_in_seq == 0

    @pl.when(pl.program_id(0) == 0)
    def _():
        kp = lax.broadcasted_iota(jnp.int32, (2 * BLOCK, 2 * BLOCK), 0)
        lane = lax.broadcasted_iota(jnp.int32, (2 * BLOCK, 2 * BLOCK), 1)
        rel = (lane & (BLOCK - 1)) + BLOCK - kp
        band = (rel >= 0) & (rel < WINDOW)
        band_ref[0] = jnp.where(band, 1.0, 0.0)
        band_ref[1] = jnp.where(band & (kp >= BLOCK), 1.0, 0.0)

    @pl.when(first)
    def _():
        k_ref[:, 0:BLOCK, :] = jnp.zeros((4, BLOCK, LANES), BF16)
        vt_ref[:, 0:BLOCK] = jnp.zeros((KV_WIDTH, BLOCK), BF16)
        p_ref[0:POOL_HALO, :] = jnp.zeros((POOL_HALO, POOL_WIDTH), F32)

    x = x_ref[...]
    h = (_normalize(x) * (1.0 + sc_ref[...]) + sh_ref[...]).astype(BF16)
    proj = jnp.dot(h, win_ref[...], preferred_element_type=F32)

    n_blocks = tm // BLOCK
    for j in range(n_blocks):
        for pair in range(ATTN_HEADS // 2):
            q_ref[j, pair // 2, (pair % 2) * BLOCK:(pair % 2 + 1) * BLOCK, :] = (
                proj[j * BLOCK:(j + 1) * BLOCK, pair * LANES:(pair + 1) * LANES] * (HEAD_DIM ** -0.5)).astype(BF16)
    k = proj[:, ATTN_WIDTH:ATTN_WIDTH + KV_WIDTH]
    k_swapped = pltpu.roll(k, HEAD_DIM, 1)
    low_half = lax.broadcasted_iota(jnp.int32, (tm, LANES), 1) < HEAD_DIM
    k_ref[0, BLOCK:, :] = jnp.where(low_half, k, 0.0).astype(BF16)
    k_ref[1, BLOCK:, :] = jnp.where(low_half, 0.0, k_swapped).astype(BF16)
    k_ref[2, BLOCK:, :] = jnp.where(low_half, k_swapped, 0.0).astype(BF16)
    k_ref[3, BLOCK:, :] = jnp.where(low_half, 0.0, k).astype(BF16)
    vt_ref[:, BLOCK:] = proj[:, ATTN_WIDTH + KV_WIDTH:ATTN_WIDTH + 2 * KV_WIDTH].T.astype(BF16)
    p_ref[POOL_HALO:, :] = proj[:, ATTN_WIDTH + 2 * KV_WIDTH:]

    second_pair = lax.broadcasted_iota(jnp.int32, (1, 2 * BLOCK), 1) >= BLOCK
    contract_last = (((1,), (1,)), ((), ()))
    steps = [(j, kv, parity) for j in range(n_blocks) for kv in range(KV_HEADS) for parity in range(2)]

    def scores(i):
        j, kv, parity = steps[i]
        s_ref[i % 2] = lax.dot_general(k_ref[2 * kv + parity, j * BLOCK:(j + 2) * BLOCK, :], q_ref[j, kv],
                                       contract_last, preferred_element_type=F32)

    def masked_chunk(i, band, c):
        rows = slice(c, c + SOFTMAX_ROWS)
        return jnp.where(band[rows, :] > 0.0, s_ref[i % 2, rows, :], NEG_INF)

    def fold_rows(op, chunk):
        return functools.reduce(op, [chunk[r:r + SUBLANES] for r in range(0, SOFTMAX_ROWS, SUBLANES)])

    scores(0)
    outs = []
    for i, (j, kv, parity) in enumerate(steps):
        if i + 1 < len(steps):
            scores(i + 1)
        band = band_ref.at[jnp.where(first, 1, 0)] if j == 0 else band_ref.at[0]
        sink = jnp.where(second_pair, sinks_ref[4 * kv + 2 + parity], sinks_ref[4 * kv + parity])
        top = jnp.full((SUBLANES, 2 * BLOCK), NEG_INF, F32)
        for c in range(0, 2 * BLOCK, SOFTMAX_ROWS):
            top = jnp.maximum(top, fold_rows(jnp.maximum, masked_chunk(i, band, c)))
        m = jnp.maximum(jnp.max(top, axis=0, keepdims=True), sink)
        acc = jnp.zeros((SUBLANES, 2 * BLOCK), F32)
        for c in range(0, 2 * BLOCK, SOFTMAX_ROWS):
            p = jnp.exp(masked_chunk(i, band, c) - m)
            acc = acc + fold_rows(jnp.add, p)
            pt_ref[i % 2, c:c + SOFTMAX_ROWS, :] = p.astype(BF16)
        denom = jnp.sum(acc, axis=0, keepdims=True) + jnp.exp(sink - m)
        out = jnp.dot(vt_ref[kv * HEAD_DIM:(kv + 1) * HEAD_DIM, j * BLOCK:(j + 2) * BLOCK], pt_ref[i % 2],
                      preferred_element_type=F32)
        outs.append(out * (1.0 / denom))
        if parity == 1:
            for half in range(2):
                pair = 2 * kv + half
                both = jnp.concatenate([o[:, half * BLOCK:(half + 1) * BLOCK] for o in outs], axis=0)
                cat_ref[j * BLOCK:(j + 1) * BLOCK, pair * LANES:(pair + 1) * LANES] = both.T.astype(BF16)
            outs = []

    pos = tile_in_seq * tm + lax.broadcasted_iota(jnp.int32, (tm, POOL_GROUP_DIM), 0)
    rows_all = POOL_HALO + tm
    for g, w in enumerate(POOL_WINDOWS):
        cols = slice(g * POOL_GROUP_DIM, (g + 1) * POOL_GROUP_DIM)
        levels = w.bit_length() - 1
        read = lambda start, back: p_ref[start - back:rows_all - back, cols]
        for lv in range(1, levels):
            first_row = SUBLANES * lv
            lvl_ref[lv % 2, first_row:rows_all, :] = read(first_row, 0) + read(first_row, 1 << (lv - 1))
            read = lambda start, back, slot=lv % 2: lvl_ref[slot, start - back:rows_all - back, :]
        total = read(POOL_HALO, 0) + read(POOL_HALO, w // 2)
        inv_count = 1.0 / jnp.minimum(pos + 1, w).astype(F32)
        pooled = (total * inv_count - p_ref[POOL_HALO:, cols]).astype(BF16)
        mixed = jnp.dot(pooled, wmix_ref[g], preferred_element_type=F32) * pscale_ref[:, cols]
        cat_ref[:, ATTN_WIDTH + g * POOL_GROUP_DIM:ATTN_WIDTH + (g + 1) * POOL_GROUP_DIM] = mixed.astype(BF16)

    k_ref[:, 0:BLOCK, :] = k_ref[:, tm:tm + BLOCK, :]
    vt_ref[:, 0:BLOCK] = vt_ref[:, tm:tm + BLOCK]
    p_ref[0:POOL_HALO, :] = p_ref[tm:tm + POOL_HALO, :]

    y = jnp.dot(cat_ref[...], wout_ref[...], preferred_element_type=F32)
    z = ALPHA * x + g_ref[...] * y
    o_ref[...] = _normalize(z) * lng_ref[...] + lnb_ref[...]


def _mix_scratch_shapes(tm, d):
    return [
        pltpu.VMEM((tm // BLOCK, KV_HEADS, 2 * BLOCK, LANES), BF16),
        pltpu.VMEM((4, BLOCK + tm, LANES), BF16),
        pltpu.VMEM((KV_WIDTH, BLOCK + tm), BF16),
        pltpu.VMEM((2, 2 * BLOCK, 2 * BLOCK), F32),
        pltpu.VMEM((2, 2 * BLOCK, 2 * BLOCK), BF16),
        pltpu.VMEM((2, 2 * BLOCK, 2 * BLOCK), F32),
        pltpu.VMEM((POOL_HALO + tm, POOL_WIDTH), F32),
        pltpu.VMEM((2, POOL_HALO + tm, POOL_GROUP_DIM), F32),
        pltpu.VMEM((tm, d), BF16),
    ]


def _mix_vmem_bytes(tm, d):
    io = 2 * 2 * tm * d * 4
    weights = (d * IN_WIDTH + d * d + POOL_GROUPS * POOL_GROUP_DIM * POOL_GROUP_DIM) * 2
    scratch = (tm * ATTN_WIDTH * 2 + 5 * (BLOCK + tm) * LANES * 2 + (POOL_HALO + tm) * (POOL_WIDTH + 2 * LANES) * 4
               + tm * d * 2 + 2 * (2 * BLOCK) ** 2 * (4 + 2 + 4))
    temps = tm * d * (2 + 4 + 4) + tm * IN_WIDTH * 4 + 4 * tm * LANES * 4
    return io + weights + scratch + temps


def _token_mix(x, shift, scale, gate, w_in, sinks, w_mix, pool_scale, w_out, ln_g, ln_b, seq):
    n, d = x.shape
    tm = ROW_TILE
    assert n % tm == 0 and seq % tm == 0 and tm % BLOCK == 0
    tiles_per_batch = seq // tm
    vmem = min(_mix_vmem_bytes(tm, d) * 3 // 2, V7X_VMEM_BYTES * 7 // 8)
    row = pl.BlockSpec((tm, d), lambda i: (i, 0))
    vec = _per_batch(tiles_per_batch)
    return pl.pallas_call(
        functools.partial(_mix_kernel, tiles_per_batch),
        grid=(n // tm,),
        in_specs=[row, vec, vec, vec, _resident((d, IN_WIDTH)),
                  pl.BlockSpec(memory_space=pltpu.SMEM),
                  _resident((POOL_GROUPS, POOL_GROUP_DIM, POOL_GROUP_DIM)), _resident((1, POOL_WIDTH)),
                  _resident((d, d)), _resident((1, d)), _resident((1, d))],
        out_specs=row,
        out_shape=jax.ShapeDtypeStruct((n, d), F32),
        scratch_shapes=_mix_scratch_shapes(tm, d),
        compiler_params=pltpu.CompilerParams(dimension_semantics=("arbitrary",), vmem_limit_bytes=vmem),
        name="token_mix",
    )(x, shift, scale, gate, w_in, sinks, w_mix, pool_scale, w_out, ln_g, ln_b)


def kernel(x, c, w_ada, b_ada, ffn1_w_gate, ffn1_w_up, ffn1_w_down, ln1_g, ln1_b, w_in, attn_sinks,
           pool_w_mix, pool_scale, w_out, ln2_g, ln2_b, ffn2_w_gate, ffn2_w_up, ffn2_w_down, ln3_g, ln3_b):
    b, s, d = x.shape
    x = x.reshape(b * s, d)
    for l in range(DEPTH):
        mod = _ada_modulation(c, w_ada[l], b_ada[l]).reshape(b, N_MOD, 1, d)
        sh1, sc1, g1, sh2, sc2, g2, sh3, sc3, g3 = (mod[:, i] for i in range(N_MOD))
        row = lambda a: a[l].reshape(1, -1)
        x = _ffn_half_step(x, sh1, sc1, g1, ffn1_w_gate[l].astype(BF16), ffn1_w_up[l].astype(BF16),
                           ffn1_w_down[l].astype(BF16), row(ln1_g), row(ln1_b), s)
        x = _token_mix(x, sh2, sc2, g2, w_in[l].astype(BF16), attn_sinks[l], pool_w_mix[l].astype(BF16),
                       row(pool_scale), w_out[l].astype(BF16), row(ln2_g), row(ln2_b), s)
        x = _ffn_half_step(x, sh3, sc3, g3, ffn2_w_gate[l].astype(BF16), ffn2_w_up[l].astype(BF16),
                           ffn2_w_down[l].astype(BF16), row(ln3_g), row(ln3_b), s)
    return x.reshape(b, s, d)
```
